```python
import jax, jax.numpy as jnp
from jax import lax
import numpy as np

D_MODEL = 2048
BATCH = 32
SEQ = 256
DEPTH = 4
DEC_BATCH = 4
DEC_SEQ = 1024
PAST_LEN = 256

GRID_W = 64
HEAD_DIM = 128
ATTN_WIDTH = D_MODEL // 2
ATTN_HEADS = ATTN_WIDTH // HEAD_DIM
ATTN_KV_HEADS = ATTN_HEADS // 4
GQA_GROUP = ATTN_HEADS // ATTN_KV_HEADS
KV_WIDTH = ATTN_KV_HEADS * HEAD_DIM
GLA_VAL_WIDTH = D_MODEL - ATTN_WIDTH
GLA_KEY_WIDTH = GLA_VAL_WIDTH // 2
GLA_HEADS = 8
GLA_DK = GLA_KEY_WIDTH // GLA_HEADS
GLA_DV = GLA_VAL_WIDTH // GLA_HEADS
GLA_RANK = 16
GLA_TAU = 16.0
GLA_CHUNK = 64
D_FF = 4 * D_MODEL
Q_BLOCK = 128
ROPE_THETA = 10000.0
ROPE_AXIS_DIM = HEAD_DIM // 2
ROPE_FREQS = ROPE_AXIS_DIM // 2
N_MOD = 6
PROJ_WIDTH = ATTN_WIDTH + 2 * KV_WIDTH + 2 * GLA_KEY_WIDTH + 2 * GLA_VAL_WIDTH + 2 * GLA_RANK
EPS = 1e-6

kernel_name = 'hybrid_gqa_gla_prefix_diffusion_step'


def rmsnorm(x, g):
    xf = x.astype(jnp.float32)
    y = xf * lax.rsqrt(jnp.mean(jnp.square(xf), axis=-1, keepdims=True) + EPS)
    return (y * g.astype(jnp.float32)).astype(x.dtype)


def axial_rope_tables(n_tokens):
    rows = n_tokens // GRID_W
    r = jnp.repeat(jnp.arange(rows, dtype=jnp.float32), GRID_W)
    col = jnp.tile(jnp.arange(GRID_W, dtype=jnp.float32), rows)
    inv = ROPE_THETA ** (-jnp.arange(ROPE_FREQS, dtype=jnp.float32) / ROPE_FREQS)
    ang_r = r[:, None] * inv[None, :]
    ang_c = col[:, None] * inv[None, :]
    return (jnp.cos(ang_r), jnp.sin(ang_r), jnp.cos(ang_c), jnp.sin(ang_c))


def _rotate(x, cos, sin):
    x1, x2 = x[..., :ROPE_FREQS], x[..., ROPE_FREQS:]
    cos, sin = cos[None, :, None, :], sin[None, :, None, :]
    return jnp.concatenate([x1 * cos - x2 * sin, x2 * cos + x1 * sin], axis=-1)


def apply_axial_rope(x, rope):
    cos_r, sin_r, cos_c, sin_c = rope
    xf = x.astype(jnp.float32)
    out = jnp.concatenate([_rotate(xf[..., :ROPE_AXIS_DIM], cos_r, sin_r),
                           _rotate(xf[..., ROPE_AXIS_DIM:], cos_c, sin_c)], axis=-1)
    return out.astype(x.dtype)


def block_attention(q, k, v):
    B, T, H, Dh = q.shape
    nb = T // Q_BLOCK
    qb = q.reshape(B, nb, Q_BLOCK, ATTN_KV_HEADS, GQA_GROUP, Dh).transpose(1, 0, 2, 3, 4, 5)
    kf = k.astype(jnp.float32)
    vf = v.astype(jnp.float32)
    scale = Dh ** -0.5

    def one_block(qblk):
        s = jnp.einsum('bqkgd,bskd->bkgqs', qblk.astype(jnp.float32), kf) * scale
        p = jax.nn.softmax(s, axis=-1)
        o = jnp.einsum('bkgqs,bskd->bqkgd', p, vf)
        return o.reshape(B, Q_BLOCK, H * Dh).astype(q.dtype)

    out = lax.map(one_block, qb)
    return out.transpose(1, 0, 2, 3).reshape(B, T, H * Dh)


def _gla_chunk_step(state, blk):
    q, k, v, g = blk
    b = jnp.cumsum(g, axis=2)
    o_inter = jnp.einsum('bhcd,bhde->bhce', q * jnp.exp(b), state)
    C = q.shape[2]
    lower = jnp.tril(jnp.ones((C, C), dtype=bool))[:, :, None]
    diff = b[:, :, :, None, :] - b[:, :, None, :, :]
    decay = jnp.exp(jnp.where(lower, diff, -jnp.inf))
    scores = jnp.sum(q[:, :, :, None, :] * k[:, :, None, :, :] * decay, axis=-1)
    o_intra = jnp.einsum('bhij,bhje->bhie', scores, v)
    b_last = b[:, :, -1, :]
    k_dec = k * jnp.exp(b_last[:, :, None, :] - b)
    new_state = jnp.exp(b_last)[..., None] * state + jnp.einsum('bhcd,bhce->bhde', k_dec, v)
    return new_state, o_inter + o_intra


def gla_scan(q, k, v, g, s0):
    B, T, H, _ = q.shape
    nc = T // GLA_CHUNK

    def to_chunks(a):
        return a.astype(jnp.float32).reshape(B, nc, GLA_CHUNK, H, a.shape[-1]).transpose(1, 0, 3, 2, 4)

    s_final, o = lax.scan(_gla_chunk_step, s0.astype(jnp.float32),
                          (to_chunks(q), to_chunks(k), to_chunks(v), to_chunks(g)))
    o = o.transpose(1, 0, 3, 2, 4).reshape(B, T, H, v.shape[-1])
    return o, s_final


def gla_bidirectional(q, k, v, g_fwd, g_bwd, s_f0, s_b0):
    o_f, s_f = gla_scan(q, k, v, g_fwd, s_f0)
    rev = lambda a: jnp.flip(a, axis=1)
    o_b, s_b = gla_scan(rev(q), rev(k), rev(v), rev(g_bwd), s_b0)
    return o_f + rev(o_b), s_f, s_b


def trunk_layer(x, mod, rope, ctx, norm1, w_in, q_norm, k_norm, w_gate_fwd, b_gate_fwd,
                w_gate_bwd, b_gate_bwd, gla_norm, w_out, norm2, w_up, w_down):
    B, T, _ = x.shape
    shift1, scale1, gate1, shift2, scale2, gate2 = jnp.split(mod, N_MOD, axis=-1)
    h = rmsnorm(x, norm1) * (1.0 + scale1) + shift1
    proj = h @ w_in
    sizes = [ATTN_WIDTH, KV_WIDTH, KV_WIDTH, GLA_KEY_WIDTH, GLA_KEY_WIDTH,
             GLA_VAL_WIDTH, GLA_VAL_WIDTH, GLA_RANK, GLA_RANK]
    offsets = [int(s) for s in np.cumsum(sizes)[:-1]]
    q_a, k_a, v_a, q_g, k_g, v_g, o_gate, r_f, r_b = jnp.split(proj, offsets, axis=-1)

    q_a = rmsnorm(q_a.reshape(B, T, ATTN_HEADS, HEAD_DIM), q_norm)
    k_a = rmsnorm(k_a.reshape(B, T, ATTN_KV_HEADS, HEAD_DIM), k_norm)
    v_a = v_a.reshape(B, T, ATTN_KV_HEADS, HEAD_DIM)

    g_f = jax.nn.log_sigmoid((r_f @ w_gate_fwd + b_gate_fwd).astype(jnp.float32)) / GLA_TAU
    g_b = jax.nn.log_sigmoid((r_b @ w_gate_bwd + b_gate_bwd).astype(jnp.float32)) / GLA_TAU
    q_g = q_g.reshape(B, T, GLA_HEADS, GLA_DK) * (GLA_DK ** -0.5)
    k_g = k_g.reshape(B, T, GLA_HEADS, GLA_DK)
    v_g = v_g.reshape(B, T, GLA_HEADS, GLA_DV)
    g_f = g_f.reshape(B, T, GLA_HEADS, GLA_DK)
    g_b = g_b.reshape(B, T, GLA_HEADS, GLA_DK)

    if ctx is None:
        keys, vals = k_a, v_a
        s_f0 = jnp.zeros((B, GLA_HEADS, GLA_DK, GLA_DV), jnp.float32)
        s_b0 = jnp.zeros((B, GLA_HEADS, GLA_DK, GLA_DV), jnp.float32)
    else:
        k_ctx, v_ctx, s_f0, s_b0 = ctx
        q_a = apply_axial_rope(q_a, rope)
        k_lat = apply_axial_rope(k_a, rope)
        keys = jnp.concatenate([k_ctx.astype(k_lat.dtype), k_lat], axis=1)
        vals = jnp.concatenate([v_ctx.astype(v_a.dtype), v_a], axis=1)

    attn_out = block_attention(q_a, keys, vals)
    o_gla, s_f, s_b = gla_bidirectional(q_g, k_g, v_g, g_f, g_b, s_f0, s_b0)
    o_gla = (rmsnorm(o_gla, gla_norm).reshape(B, T, GLA_VAL_WIDTH) * jax.nn.silu(o_gate)).astype(x.dtype)

    mix = jnp.concatenate([attn_out, o_gla], axis=-1) @ w_out
    x = x + gate1 * mix
    h2 = rmsnorm(x, norm2) * (1.0 + scale2) + shift2
    x = x + gate2 * (jnp.square(jax.nn.relu(h2 @ w_up)) @ w_down)
    return x, (k_a, v_a, s_f, s_b)


def setup_inputs(seed: int = 0) -> dict:
    key = jax.random.key(seed)
    ks = jax.random.split(key, 24)
    f32 = jnp.float32
    n = lambda i, shape, s: jax.random.normal(ks[i], shape, f32) * s
    return {
        'x_prompt': n(0, (BATCH, SEQ, D_MODEL), 1.0),
        'x_sample': n(1, (DEC_BATCH, DEC_SEQ, D_MODEL), 1.0),
        'cache_k': n(2, (DEC_BATCH, DEPTH, PAST_LEN, ATTN_KV_HEADS, HEAD_DIM), 1.0),
        'cache_v': n(3, (DEC_BATCH, DEPTH, PAST_LEN, ATTN_KV_HEADS, HEAD_DIM), 1.0),
        'state_gla_fwd': n(4, (DEC_BATCH, DEPTH, GLA_HEADS, GLA_DK, GLA_DV), 0.5),
        'state_gla_bwd': n(5, (DEC_BATCH, DEPTH, GLA_HEADS, GLA_DK, GLA_DV), 0.5),
        'c': n(6, (DEC_BATCH, D_MODEL), 1.0),
        'c_ctx': n(7, (D_MODEL,), 1.0),
        'w_mod': n(8, (DEPTH, D_MODEL, N_MOD * D_MODEL), D_MODEL ** -0.5),
        'b_mod': n(9, (DEPTH, N_MOD * D_MODEL), 0.02),
        'norm1': 1.0 + n(10, (DEPTH, D_MODEL), 0.02),
        'w_in': n(11, (DEPTH, D_MODEL, PROJ_WIDTH), D_MODEL ** -0.5),
        'q_norm': 1.0 + n(12, (DEPTH, HEAD_DIM), 0.02),
        'k_norm': 1.0 + n(13, (DEPTH, HEAD_DIM), 0.02),
        'w_gate_fwd': n(14, (DEPTH, GLA_RANK, GLA_KEY_WIDTH), GLA_RANK ** -0.5),
        'b_gate_fwd': n(15, (DEPTH, GLA_KEY_WIDTH), 0.1),
        'w_gate_bwd': n(16, (DEPTH, GLA_RANK, GLA_KEY_WIDTH), GLA_RANK ** -0.5),
        'b_gate_bwd': n(17, (DEPTH, GLA_KEY_WIDTH), 0.1),
        'gla_norm': 1.0 + n(18, (DEPTH, GLA_DV), 0.02),
        'w_out': n(19, (DEPTH, D_MODEL, D_MODEL), D_MODEL ** -0.5),
        'norm2': 1.0 + n(20, (DEPTH, D_MODEL), 0.02),
        'w_up': n(21, (DEPTH, D_MODEL, D_FF), D_MODEL ** -0.5),
        'w_down': n(22, (DEPTH, D_FF, D_MODEL), D_FF ** -0.5),
    }


def reference(x_prompt, x_sample, cache_k, cache_v, state_gla_fwd, state_gla_bwd, c, c_ctx,
              w_mod, b_mod, norm1, w_in, q_norm, k_norm, w_gate_fwd, b_gate_fwd,
              w_gate_bwd, b_gate_bwd, gla_norm, w_out, norm2, w_up, w_down):
    rope = axial_rope_tables(x_sample.shape[1])
    silu_ctx = jax.nn.silu(c_ctx)
    silu_c = jax.nn.silu(c)
    y_p, y_s = x_prompt, x_sample
    ks, vs, sfs, sbs = [], [], [], []
    for l in range(DEPTH):
        lw = (norm1[l], w_in[l], q_norm[l], k_norm[l], w_gate_fwd[l], b_gate_fwd[l],
              w_gate_bwd[l], b_gate_bwd[l], gla_norm[l], w_out[l], norm2[l], w_up[l], w_down[l])
        mod_ctx = silu_ctx @ w_mod[l] + b_mod[l]
        y_p, (k_l, v_l, sf_l, sb_l) = trunk_layer(y_p, mod_ctx, None, None, *lw)
        ks.append(k_l)
        vs.append(v_l)
        sfs.append(sf_l)
        sbs.append(sb_l)
        mod_lat = (silu_c @ w_mod[l] + b_mod[l])[:, None, :]
        ctx = (cache_k[:, l], cache_v[:, l], state_gla_fwd[:, l], state_gla_bwd[:, l])
        y_s, _ = trunk_layer(y_s, mod_lat, rope, ctx, *lw)
    new_cache_k = jnp.stack(ks, axis=1)
    new_cache_v = jnp.stack(vs, axis=1)
    new_state_gla_fwd = jnp.stack(sfs, axis=1)
    new_state_gla_bwd = jnp.stack(sbs, axis=1)
    return (y_p, y_s, new_cache_k, new_cache_v, new_state_gla_fwd, new_state_gla_bwd)
```

```python
import functools

import jax
import jax.numpy as jnp
from jax import lax
from jax.experimental import pallas as pl
from jax.experimental.pallas import tpu as pltpu

F32 = jnp.float32
BF16 = jnp.bfloat16

D_MODEL = 2048
DEPTH = 4
N_CTX_SEQ, T_CTX = 32, 256
N_LAT_SEQ, T_LAT = 4, 1024
PAST_LEN = 256
M_CTX = N_CTX_SEQ * T_CTX
M_LAT = N_LAT_SEQ * T_LAT
M_ALL = M_CTX + M_LAT

GRID_W = 64
HEAD_DIM = 128
ATTN_WIDTH = 1024
ATTN_HEADS = 8
KV_HEADS = 2
GQA_GROUP = 4
KV_WIDTH = 256
GLA_KEY_WIDTH = 512
GLA_VAL_WIDTH = 1024
GLA_HEADS = 8
GLA_DK = 64
GLA_DV = 128
GLA_RANK = 16
GLA_TAU = 16.0
D_FF = 4 * D_MODEL
ROPE_THETA = 10000.0
ROPE_FREQS = 32
N_MOD = 6
PROJ_MAIN = ATTN_WIDTH + 2 * KV_WIDTH + 2 * GLA_KEY_WIDTH + 2 * GLA_VAL_WIDTH
EPS = 1e-6

V7X_VMEM_BYTES = 64 * 1024 * 1024
VMEM_LIMIT = V7X_VMEM_BYTES - 6 * 1024 * 1024

TM = 1024
N_CTX_TILES = M_CTX // TM
N_TILES = M_ALL // TM
TN = 512
TF = 512
ROWS_SUB = 256
MOD_TN = 1024

CHUNK = 64
N_CHUNKS = TM // CHUNK
CHUNKS_PER_CTX_SEQ = T_CTX // CHUNK
HEADS_PER_GROUP = 4
GROUP_K = HEADS_PER_GROUP * GLA_DK
GROUP_V = HEADS_PER_GROUP * GLA_DV
LEVELS = (1, 2, 4, 8, 16, 32)

ATT_Q = 256
N_ATT_CTX = M_CTX // ATT_Q
N_ATT_LAT = M_LAT // ATT_Q
ATT_SCALE = HEAD_DIM ** -0.5


def _mod_row(i):
    return jnp.where(i < N_CTX_TILES, 0, i - (N_CTX_TILES - 1))


def _rms(x, g):
    return x * lax.rsqrt(jnp.mean(x * x, axis=-1, keepdims=True) + EPS) * g


def _dot(a, b):
    return jnp.dot(a, b, preferred_element_type=F32)


def _dot_nt(a, b):
    return lax.dot_general(a, b, (((1,), (1,)), ((), ())), preferred_element_type=F32)


def _dot_tn(a, b):
    return lax.dot_general(a, b, (((0,), (0,)), ((), ())), preferred_element_type=F32)


def _mod_kernel(c_ref, w_ref, b_ref, o_ref):
    c = c_ref[...]
    s = (c * jax.nn.sigmoid(c)).astype(BF16)
    o_ref[...] = _dot(s, w_ref[...].astype(BF16)) + b_ref[...]


def _modulation(cvec, w_mod, b_mod):
    return pl.pallas_call(
        _mod_kernel,
        grid=(DEPTH, N_MOD * D_MODEL // MOD_TN),
        in_specs=[
            pl.BlockSpec((8, D_MODEL), lambda l, j: (0, 0)),
            pl.BlockSpec((None, D_MODEL, MOD_TN), lambda l, j: (l, 0, j)),
            pl.BlockSpec((None, 1, MOD_TN), lambda l, j: (l, 0, j)),
        ],
        out_specs=pl.BlockSpec((None, 8, MOD_TN), lambda l, j: (l, 0, j)),
        out_shape=jax.ShapeDtypeStruct((DEPTH, 8, N_MOD * D_MODEL), F32),
        compiler_params=pltpu.CompilerParams(
            dimension_semantics=("arbitrary", "arbitrary"), vmem_limit_bytes=VMEM_LIMIT),
        name="modulation",
    )(cvec, w_mod, b_mod.reshape(DEPTH, 1, N_MOD * D_MODEL))


def _norm_mod_to(h_ref, x_ref, mod_ref, g_ref, row, shift_off, scale_off):
    shift = mod_ref[pl.ds(row, 1), shift_off:shift_off + D_MODEL]
    scale = mod_ref[pl.ds(row, 1), scale_off:scale_off + D_MODEL]
    g = g_ref[...]

    def body(t, carry):
        r0 = pl.multiple_of(t * ROWS_SUB, ROWS_SUB)
        x = x_ref[pl.ds(r0, ROWS_SUB), :]
        h_ref[pl.ds(r0, ROWS_SUB), :] = (_rms(x, g) * (1.0 + scale) + shift).astype(BF16)
        return carry

    lax.fori_loop(0, TM // ROWS_SUB, body, 0)


def _in_proj_kernel(x_ref, mod_ref, g_ref, w_ref, wr_ref, o_ref, r_ref, h_ref):
    i = pl.program_id(0)
    j = pl.program_id(1)

    @pl.when(j == 0)
    def _():
        _norm_mod_to(h_ref, x_ref, mod_ref, g_ref, _mod_row(i), 0, D_MODEL)
        r_ref[...] = _dot(h_ref[...], wr_ref[...].astype(BF16))

    o_ref[...] = _dot(h_ref[...], w_ref[...].astype(BF16))


def _in_proj(x, mod_l, norm1_l, w_in, w_rank, l):
    return pl.pallas_call(
        _in_proj_kernel,
        grid=(N_TILES, PROJ_MAIN // TN),
        in_specs=[
            pl.BlockSpec((TM, D_MODEL), lambda i, j: (i, 0)),
            pl.BlockSpec((8, N_MOD * D_MODEL), lambda i, j: (0, 0)),
            pl.BlockSpec((1, D_MODEL), lambda i, j: (0, 0)),
            pl.BlockSpec((None, D_MODEL, TN), lambda i, j: (l, 0, j)),
            pl.BlockSpec((None, D_MODEL, 2 * GLA_RANK), lambda i, j: (l, 0, 0)),
        ],
        out_specs=[
            pl.BlockSpec((TM, TN), lambda i, j: (i, j)),
            pl.BlockSpec((TM, 2 * GLA_RANK), lambda i, j: (i, 0)),
        ],
        out_shape=[
            jax.ShapeDtypeStruct((M_ALL, PROJ_MAIN), F32),
            jax.ShapeDtypeStruct((M_ALL, 2 * GLA_RANK), F32),
        ],
        scratch_shapes=[pltpu.VMEM((TM, D_MODEL), BF16)],
        compiler_params=pltpu.CompilerParams(
            dimension_semantics=("arbitrary", "arbitrary"), vmem_limit_bytes=VMEM_LIMIT),
        name="in_proj",
    )(x, mod_l, norm1_l, w_in, w_rank)


def _out_proj_kernel(a_ref, g_ref, x_ref, mod_ref, w_ref, o_ref):
    i = pl.program_id(0)
    j = pl.program_id(1)
    gate_row = mod_ref[pl.ds(_mod_row(i), 1), 2 * D_MODEL:3 * D_MODEL]
    gate = gate_row[:, :TN]
    for jj in range(1, D_MODEL // TN):
        gate = jnp.where(j == jj, gate_row[:, jj * TN:(jj + 1) * TN], gate)
    w = w_ref[...]
    mix = (_dot(a_ref[...], w[:ATTN_WIDTH].astype(BF16))
           + _dot(g_ref[...], w[ATTN_WIDTH:].astype(BF16)))
    o_ref[...] = x_ref[...] + gate * mix


def _out_proj(attn, gla, x, mod_l, w_out, l):
    return pl.pallas_call(
        _out_proj_kernel,
        grid=(N_TILES, D_MODEL // TN),
        in_specs=[
            pl.BlockSpec((TM, ATTN_WIDTH), lambda i, j: (i, 0)),
            pl.BlockSpec((TM, GLA_VAL_WIDTH), lambda i, j: (i, 0)),
            pl.BlockSpec((TM, TN), lambda i, j: (i, j)),
            pl.BlockSpec((8, N_MOD * D_MODEL), lambda i, j: (0, 0)),
            pl.BlockSpec((None, D_MODEL, TN), lambda i, j: (l, 0, j)),
        ],
        out_specs=pl.BlockSpec((TM, TN), lambda i, j: (i, j)),
        out_shape=jax.ShapeDtypeStruct((M_ALL, D_MODEL), F32),
        compiler_params=pltpu.CompilerParams(
            dimension_semantics=("arbitrary", "arbitrary"), vmem_limit_bytes=VMEM_LIMIT),
        name="out_proj",
    )(attn, gla, x, mod_l, w_out)


def _mlp_kernel(x_ref, mod_ref, g_ref, wu_ref, wd_ref, o_ref, h_ref):
    i = pl.program_id(0)
    f = pl.program_id(1)
    row = _mod_row(i)

    @pl.when(f == 0)
    def _():
        _norm_mod_to(h_ref, x_ref, mod_ref, g_ref, row, 3 * D_MODEL, 4 * D_MODEL)

    u = _dot(h_ref[...], wu_ref[...].astype(BF16))
    u = jnp.square(jnp.maximum(u, 0.0)).astype(BF16)
    d = _dot(u, wd_ref[...].astype(BF16))

    @pl.when(f == 0)
    def _():
        o_ref[...] = d

    @pl.when(f > 0)
    def _():
        o_ref[...] += d

    @pl.when(f == D_FF // TF - 1)
    def _():
        gate = mod_ref[pl.ds(row, 1), 5 * D_MODEL:6 * D_MODEL]
        o_ref[...] = x_ref[...] + gate * o_ref[...]


def _mlp(x, mod_l, norm2_l, w_up, w_down, l):
    return pl.pallas_call(
        _mlp_kernel,
        grid=(N_TILES, D_FF // TF),
        in_specs=[
            pl.BlockSpec((TM, D_MODEL), lambda i, f: (i, 0), pipeline_mode=pl.Buffered(1)),
            pl.BlockSpec((8, N_MOD * D_MODEL), lambda i, f: (0, 0)),
            pl.BlockSpec((1, D_MODEL), lambda i, f: (0, 0)),
            pl.BlockSpec((None, D_MODEL, TF), lambda i, f: (l, 0, f)),
            pl.BlockSpec((None, TF, D_MODEL), lambda i, f: (l, f, 0)),
        ],
        out_specs=pl.BlockSpec((TM, D_MODEL), lambda i, f: (i, 0)),
        out_shape=jax.ShapeDtypeStruct((M_ALL, D_MODEL), F32),
        scratch_shapes=[pltpu.VMEM((TM, D_MODEL), BF16)],
        compiler_params=pltpu.CompilerParams(
            dimension_semantics=("arbitrary", "arbitrary"), vmem_limit_bytes=VMEM_LIMIT),
        name="mlp",
    )(x, mod_l, norm2_l, w_up, w_down)


def _rope(x, cos, sin_signed):
    lane = lax.broadcasted_iota(jnp.int32, x.shape, 1)
    first = (lane & 63) < ROPE_FREQS
    partner = jnp.where(first, pltpu.roll(x, HEAD_DIM - ROPE_FREQS, 1), pltpu.roll(x, ROPE_FREQS, 1))
    return x * cos + partner * sin_signed


def _softmax_pv(scores, values):
    m = functools.reduce(jnp.maximum, [jnp.max(s, axis=-1, keepdims=True) for s in scores])
    ps = [jnp.exp(s - m) for s in scores]
    denom = functools.reduce(jnp.add, [jnp.sum(p, axis=-1, keepdims=True) for p in ps])
    acc = functools.reduce(jnp.add, [_dot(p.astype(BF16), v) for p, v in zip(ps, values)])
    return acc / denom


def _attn_kernel(q_ref, kvc_ref, kvl_ref, ck_ref, cv_ref, qn_ref, kn_ref, cos_ref, sin_ref,
                 o_ref, knew_ref, klat_ref, vlat_ref):
    s = pl.program_id(0)
    qn = qn_ref[...]
    kn = kn_ref[...]

    def head(x, h):
        return x[:, h * HEAD_DIM:(h + 1) * HEAD_DIM]

    def store_heads(o, g2):
        for a in range(GQA_GROUP):
            h = g2 * GQA_GROUP + a
            o_ref[:, h * HEAD_DIM:(h + 1) * HEAD_DIM] = o[a * ATT_Q:(a + 1) * ATT_Q].astype(BF16)

    @pl.when(s < N_ATT_CTX)
    def _():
        q = q_ref[...]
        kv = kvc_ref[...]
        for g2 in range(KV_HEADS):
            kh = _rms(head(kv, g2), kn)
            knew_ref[:, g2 * HEAD_DIM:(g2 + 1) * HEAD_DIM] = kh
            vh = head(kv, KV_HEADS + g2).astype(BF16)
            qs = jnp.concatenate(
                [_rms(head(q, g2 * GQA_GROUP + a), qn) * ATT_SCALE for a in range(GQA_GROUP)],
                axis=0).astype(BF16)
            sc = _dot_nt(qs, kh.astype(BF16))
            store_heads(_softmax_pv([sc], [vh]), g2)

    @pl.when(s >= N_ATT_CTX)
    def _():
        qb = (s - N_ATT_CTX) % (T_LAT // ATT_Q)

        @pl.when(qb == 0)
        def _():
            kv = kvl_ref[...]
            cos = cos_ref[...]
            sin = sin_ref[...]
            for g2 in range(KV_HEADS):
                kh = _rope(_rms(head(kv, g2), kn), cos, sin)
                klat_ref[:, g2 * HEAD_DIM:(g2 + 1) * HEAD_DIM] = kh.astype(BF16)
            vlat_ref[...] = kv[:, KV_WIDTH:].astype(BF16)

        q0 = pl.multiple_of(qb * ATT_Q, ATT_Q)
        cos = cos_ref[pl.ds(q0, ATT_Q), :]
        sin = sin_ref[pl.ds(q0, ATT_Q), :]
        q = q_ref[...]
        for g2 in range(KV_HEADS):
            qs = jnp.concatenate(
                [_rope(_rms(head(q, g2 * GQA_GROUP + a), qn), cos, sin) * ATT_SCALE
                 for a in range(GQA_GROUP)], axis=0).astype(BF16)
            ck = head(ck_ref[...], g2).astype(BF16)
            cv = head(cv_ref[...], g2).astype(BF16)
            kl = klat_ref[:, g2 * HEAD_DIM:(g2 + 1) * HEAD_DIM]
            vl = vlat_ref[:, g2 * HEAD_DIM:(g2 + 1) * HEAD_DIM]
            store_heads(_softmax_pv([_dot_nt(qs, ck), _dot_nt(qs, kl)], [cv, vl]), g2)


def _lat_seq(s):
    return jnp.maximum(s - N_ATT_CTX, 0) // (T_LAT // ATT_Q)


def _attention(proj, cache_k, cache_v, q_norm_l, k_norm_l, cos, sin, l):
    kv_col = (ATTN_WIDTH) // (2 * KV_WIDTH)
    return pl.pallas_call(
        _attn_kernel,
        grid=(N_ATT_CTX + N_ATT_LAT,),
        in_specs=[
            pl.BlockSpec((ATT_Q, ATTN_WIDTH), lambda s: (s, 0)),
            pl.BlockSpec((T_CTX, 2 * KV_WIDTH), lambda s: (jnp.minimum(s, N_ATT_CTX - 1), kv_col)),
            pl.BlockSpec((T_LAT, 2 * KV_WIDTH), lambda s: (M_CTX // T_LAT + _lat_seq(s), kv_col)),
            pl.BlockSpec((None, None, PAST_LEN, KV_WIDTH), lambda s: (_lat_seq(s), l, 0, 0)),
            pl.BlockSpec((None, None, PAST_LEN, KV_WIDTH), lambda s: (_lat_seq(s), l, 0, 0)),
            pl.BlockSpec((1, HEAD_DIM), lambda s: (0, 0)),
            pl.BlockSpec((1, HEAD_DIM), lambda s: (0, 0)),
            pl.BlockSpec((T_LAT, HEAD_DIM), lambda s: (0, 0)),
            pl.BlockSpec((T_LAT, HEAD_DIM), lambda s: (0, 0)),
        ],
        out_specs=[
            pl.BlockSpec((ATT_Q, ATTN_WIDTH), lambda s: (s, 0)),
            pl.BlockSpec((T_CTX, KV_WIDTH), lambda s: (jnp.minimum(s, N_ATT_CTX - 1), 0)),
        ],
        out_shape=[
            jax.ShapeDtypeStruct((M_ALL, ATTN_WIDTH), BF16),
            jax.ShapeDtypeStruct((M_CTX, KV_WIDTH), F32),
        ],
        scratch_shapes=[pltpu.VMEM((T_LAT, KV_WIDTH), BF16), pltpu.VMEM((T_LAT, KV_WIDTH), BF16)],
        compiler_params=pltpu.CompilerParams(
            dimension_semantics=("arbitrary",), vmem_limit_bytes=VMEM_LIMIT),
        name="attention",
    )(proj, proj, proj, cache_k, cache_v, q_norm_l, k_norm_l, cos, sin)


def _block_total(cs, h, row):
    if h >= 8:
        parts = []
        for b in range(CHUNK // (2 * h)):
            src = cs[b * 2 * h + h - 1:b * 2 * h + h, :]
            parts.append(jnp.broadcast_to(src, (2 * h, cs.shape[1])))
        return parts[0] if len(parts) == 1 else jnp.concatenate(parts, axis=0)
    z = jnp.where((row & (2 * h - 1)) == h - 1, cs, 0.0)
    if h == 1:
        return z + pltpu.roll(z, 1, 0)
    if h == 2:
        z = z + pltpu.roll(z, CHUNK - 1, 0)
        return z + pltpu.roll(z, 2, 0)
    z = z + pltpu.roll(z, CHUNK - 1, 0)
    z = z + pltpu.roll(z, CHUNK - 2, 0)
    return z + pltpu.roll(z, 4, 0)


def _gla_chunk(c, backward, refs):
    (q_ref, k_ref, v_refs, r_ref, wg_ref, bg_ref, oacc_ref, st_ref, masks) = refs
    hm, hmv, mask_t, ii, jj = masks
    r0 = pl.multiple_of(c * CHUNK, CHUNK)
    x = _dot(r_ref[pl.ds(r0, CHUNK), :].astype(BF16), wg_ref[...]) + bg_ref[...]
    g = (jnp.minimum(x, 0.0) - jnp.log(1.0 + jnp.exp(-jnp.abs(x)))) * (1.0 / GLA_TAU)

    row = lax.broadcasted_iota(jnp.int32, g.shape, 0)
    cs = g
    es = []
    for h in LEVELS:
        tot_h = _block_total(cs, h, row)
        upper = (row & h) != 0
        a = cs - g if backward else cs
        es.append(jnp.exp(jnp.where(upper, a, tot_h - a)))
        cs = cs + jnp.where(upper, tot_h, 0.0)
    tot = cs[CHUNK - 1:CHUNK, :]
    a = cs - g if backward else cs
    e_query = jnp.exp(tot - a) if backward else jnp.exp(a)
    e_key = jnp.exp(a) if backward else jnp.exp(tot - a)
    state_decay = jnp.exp(tot)

    q = q_ref[pl.ds(r0, CHUNK), :] * (GLA_DK ** -0.5)
    k = k_ref[pl.ds(r0, CHUNK), :]
    lvl = ii ^ jj
    keep = (jj >= ii) if backward else (jj <= ii)
    d = 1 if backward else 0

    for grp in range(GLA_HEADS // HEADS_PER_GROUP):
        sl = slice(grp * GROUP_K, (grp + 1) * GROUP_K)
        qg, kg = q[:, sl], k[:, sl]
        vb = v_refs[grp][pl.ds(r0, CHUNK), :].astype(BF16)

        def scores(qq, kk):
            kbd = jnp.concatenate([kk.astype(BF16)] * HEADS_PER_GROUP, axis=0) * hm
            return _dot_nt(qq.astype(BF16), kbd)

        att = scores(qg, kg)
        for idx, h in enumerate(LEVELS):
            e = es[idx][:, sl]
            att = jnp.where(lvl >= h, scores(qg * e, kg * e), att)
        att = jnp.where(keep, att, 0.0)

        vbd = jnp.concatenate([vb] * HEADS_PER_GROUP, axis=0) * hmv
        st = st_ref[d, grp]
        o = _dot(att.astype(BF16), vbd) + _dot_nt((qg * e_query[:, sl]).astype(BF16), st.astype(BF16))
        cols = slice(grp * GROUP_V, (grp + 1) * GROUP_V)
        if backward:
            oacc_ref[pl.ds(r0, CHUNK), cols] += o
        else:
            oacc_ref[pl.ds(r0, CHUNK), cols] = o

        upd = _dot_tn(vb, (kg * e_key[:, sl]).astype(BF16))
        st_ref[d, grp] = st * state_decay[:, sl] + jnp.where(mask_t, upd, 0.0)


def _store_state(out_ref, seq, st_ref, d):
    for grp in range(GLA_HEADS // HEADS_PER_GROUP):
        for hl in range(HEADS_PER_GROUP):
            blk = st_ref[d, grp, hl * GLA_DV:(hl + 1) * GLA_DV, hl * GLA_DK:(hl + 1) * GLA_DK]
            h = grp * HEADS_PER_GROUP + hl
            out_ref[seq, h * GLA_DV:(h + 1) * GLA_DV, :] = blk


def _gla_kernel(q_ref, k_ref, v0_ref, v1_ref, og0_ref, og1_ref, r_ref, wgf_ref, wgb_ref,
                bgf_ref, bgb_ref, gn_ref, s0f_ref, s0b_ref,
                o_ref, sf_ref, sb_ref, oacc_ref, st_ref, wpad_ref):
    s = pl.program_id(0)
    is_ctx = s < N_CTX_TILES

    def iota2(shape, dim):
        return lax.broadcasted_iota(jnp.int32, shape, dim)

    hm = (iota2((GROUP_K, GROUP_K), 0) // GLA_DK == iota2((GROUP_K, GROUP_K), 1) // GLA_DK).astype(BF16)
    hmv = (iota2((GROUP_K, GROUP_V), 0) // CHUNK == iota2((GROUP_K, GROUP_V), 1) // GLA_DV).astype(BF16)
    mask_t = iota2((GROUP_V, GROUP_K), 0) // GLA_DV == iota2((GROUP_V, GROUP_K), 1) // GLA_DK
    ii = iota2((CHUNK, GROUP_K), 0)
    jj = iota2((CHUNK, GROUP_K), 1) & (CHUNK - 1)
    masks = (hm, hmv, mask_t, ii, jj)

    zeros = jnp.zeros((GLA_RANK, GLA_KEY_WIDTH), F32)
    wpad_ref[0] = jnp.concatenate([wgf_ref[...], zeros], axis=0).astype(BF16)
    wpad_ref[1] = jnp.concatenate([zeros, wgb_ref[...]], axis=0).astype(BF16)

    for backward in (False, True):
        d = 1 if backward else 0
        s0_ref = s0b_ref if backward else s0f_ref
        out_ref = sb_ref if backward else sf_ref
        refs = (q_ref, k_ref, (v0_ref, v1_ref), r_ref, wpad_ref.at[d],
                bgb_ref if backward else bgf_ref, oacc_ref, st_ref, masks)

        @pl.when(jnp.logical_not(is_ctx))
        def _():
            for grp in range(GLA_HEADS // HEADS_PER_GROUP):
                init = s0_ref[grp * GROUP_V:(grp + 1) * GROUP_V, :]
                st_ref[d, grp] = jnp.where(mask_t, init, 0.0)

        def body(t, carry):
            c = (N_CHUNKS - 1 - t) if backward else t
            first_of_seq = (c % CHUNKS_PER_CTX_SEQ) == ((CHUNKS_PER_CTX_SEQ - 1) if backward else 0)
            last_of_seq = (c % CHUNKS_PER_CTX_SEQ) == (0 if backward else (CHUNKS_PER_CTX_SEQ - 1))

            @pl.when(jnp.logical_and(is_ctx, first_of_seq))
            def _():
                st_ref[d] = jnp.zeros(st_ref.shape[1:], F32)

            _gla_chunk(c, backward, refs)

            @pl.when(jnp.logical_and(is_ctx, last_of_seq))
            def _():
                _store_state(out_ref, c // CHUNKS_PER_CTX_SEQ, st_ref, d)

            return carry

        lax.fori_loop(0, N_CHUNKS, body, 0)

    gn = gn_ref[...]

    def epilogue(t, carry):
        r0 = pl.multiple_of(t * ROWS_SUB, ROWS_SUB)
        for h in range(GLA_HEADS):
            og_ref = og0_ref if h < HEADS_PER_GROUP else og1_ref
            c0 = (h % HEADS_PER_GROUP) * GLA_DV
            gate = og_ref[pl.ds(r0, ROWS_SUB), c0:c0 + GLA_DV]
            o = oacc_ref[pl.ds(r0, ROWS_SUB), h * GLA_DV:(h + 1) * GLA_DV]
            y = _rms(o, gn) * (gate * jax.nn.sigmoid(gate))
            o_ref[pl.ds(r0, ROWS_SUB), h * GLA_DV:(h + 1) * GLA_DV] = y.astype(BF16)
        return carry

    lax.fori_loop(0, TM // ROWS_SUB, epilogue, 0)


def _gla(proj, r, wgf, wgb, bgf, bgb, gn, s0f, s0b, l):
    def col(cb):
        return pl.BlockSpec((TM, TN), lambda s: (s, cb))

    base = (ATTN_WIDTH + 2 * KV_WIDTH) // TN
    seqs_per_tile = TM // T_CTX

    def lat(s):
        return jnp.maximum(s - N_CTX_TILES, 0)

    def ctx(s):
        return jnp.minimum(s, N_CTX_TILES - 1)

    return pl.pallas_call(
        _gla_kernel,
        grid=(N_TILES,),
        in_specs=[
            col(base), col(base + 1), col(base + 2), col(base + 3), col(base + 4), col(base + 5),
            pl.BlockSpec((TM, 2 * GLA_RANK), lambda s: (s, 0)),
            pl.BlockSpec((None, GLA_RANK, GLA_KEY_WIDTH), lambda s: (l, 0, 0)),
            pl.BlockSpec((None, GLA_RANK, GLA_KEY_WIDTH), lambda s: (l, 0, 0)),
            pl.BlockSpec((None, 1, GLA_KEY_WIDTH), lambda s: (l, 0, 0)),
            pl.BlockSpec((None, 1, GLA_KEY_WIDTH), lambda s: (l, 0, 0)),
            pl.BlockSpec((1, GLA_DV), lambda s: (0, 0)),
            pl.BlockSpec((None, GLA_HEADS * GLA_DV, GROUP_K), lambda s: (lat(s), 0, 0)),
            pl.BlockSpec((None, GLA_HEADS * GLA_DV, GROUP_K), lambda s: (lat(s), 0, 0)),
        ],
        out_specs=[
            pl.BlockSpec((TM, GLA_VAL_WIDTH), lambda s: (s, 0)),
            pl.BlockSpec((seqs_per_tile, GLA_HEADS * GLA_DV, GLA_DK), lambda s: (ctx(s), 0, 0)),
            pl.BlockSpec((seqs_per_tile, GLA_HEADS * GLA_DV, GLA_DK), lambda s: (ctx(s), 0, 0)),
        ],
        out_shape=[
            jax.ShapeDtypeStruct((M_ALL, GLA_VAL_WIDTH), BF16),
            jax.ShapeDtypeStruct((N_CTX_SEQ, GLA_HEADS * GLA_DV, GLA_DK), F32),
            jax.ShapeDtypeStruct((N_CTX_SEQ, GLA_HEADS * GLA_DV, GLA_DK), F32),
        ],
        scratch_shapes=[
            pltpu.VMEM((TM, GLA_VAL_WIDTH), F32),
            pltpu.VMEM((2, GLA_HEADS // HEADS_PER_GROUP, GROUP_V, GROUP_K), F32),
            pltpu.VMEM((2, 2 * GLA_RANK, GLA_KEY_WIDTH), BF16),
        ],
        compiler_params=pltpu.CompilerParams(
            dimension_semantics=("arbitrary",), vmem_limit_bytes=VMEM_LIMIT),
        name="gla",
    )(proj, proj, proj, proj, proj, proj, r, wgf, wgb, bgf, bgb, gn, s0f, s0b)


def _rope_tables():
    t = jnp.arange(T_LAT)
    inv = ROPE_THETA ** (-jnp.arange(ROPE_FREQS, dtype=F32) / ROPE_FREQS)
    ang_r = (t // GRID_W).astype(F32)[:, None] * inv[None, :]
    ang_c = (t % GRID_W).astype(F32)[:, None] * inv[None, :]
    cos = jnp.concatenate([jnp.cos(ang_r), jnp.cos(ang_r), jnp.cos(ang_c), jnp.cos(ang_c)], axis=-1)
    sin = jnp.concatenate([-jnp.sin(ang_r), jnp.sin(ang_r), -jnp.sin(ang_c), jnp.sin(ang_c)], axis=-1)
    return cos, sin


def _state_to_tiles(state_l):
    st = jnp.swapaxes(state_l, -1, -2).reshape(N_LAT_SEQ, GLA_HEADS * GLA_DV, GLA_DK)
    return jnp.tile(st, (1, 1, HEADS_PER_GROUP))


def kernel(x_prompt, x_sample, cache_k, cache_v, state_gla_fwd, state_gla_bwd, c, c_ctx, w_mod, b_mod,
           norm1, w_in, q_norm, k_norm, w_gate_fwd, b_gate_fwd, w_gate_bwd, b_gate_bwd, gla_norm,
           w_out, norm2, w_up, w_down):
    x = jnp.concatenate([x_prompt.reshape(M_CTX, D_MODEL), x_sample.reshape(M_LAT, D_MODEL)], axis=0)
    cvec = jnp.concatenate([c_ctx[None, :], c, jnp.zeros((8 - 1 - N_LAT_SEQ, D_MODEL), F32)], axis=0)
    mod = _modulation(cvec, w_mod, b_mod)
    cos, sin = _rope_tables()
    w_rank = w_in[:, :, PROJ_MAIN:]
    ck = cache_k.reshape(N_LAT_SEQ, DEPTH, PAST_LEN, KV_WIDTH)
    cv = cache_v.reshape(N_LAT_SEQ, DEPTH, PAST_LEN, KV_WIDTH)

    ks, vs, sfs, sbs = [], [], [], []
    for l in range(DEPTH):
        proj, r = _in_proj(x, mod[l], norm1[l][None, :], w_in, w_rank, l)
        attn, k_new = _attention(proj, ck, cv, q_norm[l][None, :], k_norm[l][None, :], cos, sin, l)
        gla, sf, sb = _gla(proj, r, w_gate_fwd, w_gate_bwd,
                           b_gate_fwd.reshape(DEPTH, 1, GLA_KEY_WIDTH), b_gate_bwd.reshape(DEPTH, 1, GLA_KEY_WIDTH),
                           gla_norm[l][None, :], _state_to_tiles(state_gla_fwd[:, l]),
                           _state_to_tiles(state_gla_bwd[:, l]), l)
        x = _out_proj(attn, gla, x, mod[l], w_out, l)
        x = _mlp(x, mod[l], norm2[l][None, :], w_up, w_down, l)
        ks.append(k_new.reshape(N_CTX_SEQ, T_CTX, KV_HEADS, HEAD_DIM))
        vs.append(proj[:M_CTX, ATTN_WIDTH + KV_WIDTH:ATTN_WIDTH + 2 * KV_WIDTH]
                  .reshape(N_CTX_SEQ, T_CTX, KV_HEADS, HEAD_DIM))
        sfs.append(jnp.swapaxes(sf.reshape(N_CTX_SEQ, GLA_HEADS, GLA_DV, GLA_DK), -1, -2))
        sbs.append(jnp.swapaxes(sb.reshape(N_CTX_SEQ, GLA_HEADS, GLA_DV, GLA_DK), -1, -2))

    y_p = x[:M_CTX].reshape(N_CTX_SEQ, T_CTX, D_MODEL)
    y_s = x[M_CTX:].reshape(N_LAT_SEQ, T_LAT, D_MODEL)
    return (y_p, y_s, jnp.stack(ks, axis=1), jnp.stack(vs, axis=1),
            jnp.stack(sfs, axis=1), jnp.stack(sbs, axis=1))
```

```python
import functools

import jax
import jax.numpy as jnp
from jax import lax
from jax.experimental import pallas as pl
from jax.experimental.pallas import tpu as pltpu

F32 = jnp.float32
BF16 = jnp.bfloat16

D_MODEL = 2048
DEPTH = 4
N_CTX_SEQ, T_CTX = 32, 256
N_LAT_SEQ, T_LAT = 4, 1024
PAST_LEN = 256
M_CTX = N_CTX_SEQ * T_CTX
M_LAT = N_LAT_SEQ * T_LAT
M_ALL = M_CTX + M_LAT

GRID_W = 64
HEAD_DIM = 128
ATTN_WIDTH = 1024
ATTN_HEADS = 8
KV_HEADS = 2
GQA_GROUP = 4
KV_WIDTH = 256
GLA_KEY_WIDTH = 512
GLA_VAL_WIDTH = 1024
GLA_HEADS = 8
GLA_DK = 64
GLA_DV = 128
GLA_RANK = 16
GLA_TAU = 16.0
D_FF = 4 * D_MODEL
ROPE_THETA = 10000.0
ROPE_FREQS = 32
N_MOD = 6
PROJ_MAIN = ATTN_WIDTH + 2 * KV_WIDTH + 2 * GLA_KEY_WIDTH + 2 * GLA_VAL_WIDTH
EPS = 1e-6
LOG2_E = 1.4426950408889634

V7X_VMEM_BYTES = 64 * 1024 * 1024
VMEM_LIMIT = V7X_VMEM_BYTES - 6 * 1024 * 1024

TM = 1024
N_CTX_TILES = M_CTX // TM
N_TILES = M_ALL // TM
TN_IN = 1536
TN_OUT = 1024
TF = 1024
PROJ_COL = 512
MLP_ROWS = 256
NORM_ROWS = 16
NORM_UNROLL = 8
MOD_TN = 1024

CHUNK = 64
SLAB = 8
N_CHUNKS = TM // CHUNK
CHUNKS_PER_CTX_SEQ = T_CTX // CHUNK
HEADS_PER_GROUP = 4
N_GROUPS = GLA_HEADS // HEADS_PER_GROUP
GROUP_K = HEADS_PER_GROUP * GLA_DK
GROUP_V = HEADS_PER_GROUP * GLA_DV
N_PAIRS = GLA_HEADS // 2
PAIR_K = 2 * GLA_DK
PAIR_V = 2 * GLA_DV
LEVELS = (1, 2, 4, 8, 16, 32)
GATE_ROWS = 32
GATE_UNROLL = 4

ATT_Q = 256
N_ATT_CTX = M_CTX // ATT_Q
N_ATT_LAT = M_LAT // ATT_Q
ATT_SCALE = HEAD_DIM ** -0.5


def _mod_row(i):
    return jnp.where(i < N_CTX_TILES, 0, i - (N_CTX_TILES - 1))


def _rms(x, g):
    return x * lax.rsqrt(jnp.mean(x * x, axis=-1, keepdims=True) + EPS) * g


def _dot(a, b):
    return jnp.dot(a, b, preferred_element_type=F32)


def _dot_nt(a, b):
    return lax.dot_general(a, b, (((1,), (1,)), ((), ())), preferred_element_type=F32)


def _dot_tn(a, b):
    return lax.dot_general(a, b, (((0,), (0,)), ((), ())), preferred_element_type=F32)


def _mod_kernel(c_ref, w_ref, b_ref, o_ref):
    c = c_ref[...]
    s = (c * jax.nn.sigmoid(c)).astype(BF16)
    o_ref[...] = _dot(s, w_ref[...].astype(BF16)) + b_ref[...]


def _modulation(cvec, w_mod, b_mod):
    return pl.pallas_call(
        _mod_kernel,
        grid=(DEPTH, N_MOD * D_MODEL // MOD_TN),
        in_specs=[
            pl.BlockSpec((8, D_MODEL), lambda l, j: (0, 0)),
            pl.BlockSpec((None, D_MODEL, MOD_TN), lambda l, j: (l, 0, j)),
            pl.BlockSpec((None, 1, MOD_TN), lambda l, j: (l, 0, j)),
        ],
        out_specs=pl.BlockSpec((None, 8, MOD_TN), lambda l, j: (l, 0, j)),
        out_shape=jax.ShapeDtypeStruct((DEPTH, 8, N_MOD * D_MODEL), F32),
        compiler_params=pltpu.CompilerParams(
            dimension_semantics=("arbitrary", "arbitrary"), vmem_limit_bytes=VMEM_LIMIT),
        name="modulation",
    )(cvec, w_mod, b_mod.reshape(DEPTH, 1, N_MOD * D_MODEL))


def _norm_mod_to(h_ref, x_ref, mod_ref, g_ref, row, shift_off, scale_off):
    shift = mod_ref[pl.ds(row, 1), shift_off:shift_off + D_MODEL]
    scale = mod_ref[pl.ds(row, 1), scale_off:scale_off + D_MODEL]
    gain = g_ref[...] * (1.0 + scale)

    def body(t, carry):
        r0 = pl.multiple_of(t * NORM_ROWS, NORM_ROWS)
        x = x_ref[pl.ds(r0, NORM_ROWS), :]
        inv = lax.rsqrt(jnp.mean(x * x, axis=-1, keepdims=True) + EPS)
        h_ref[pl.ds(r0, NORM_ROWS), :] = (x_ref[pl.ds(r0, NORM_ROWS), :] * inv * gain + shift).astype(BF16)
        return carry

    lax.fori_loop(0, TM // NORM_ROWS, body, 0, unroll=NORM_UNROLL)


def _in_proj_kernel(x_ref, mod_ref, g_ref, w_ref, wr_ref, o_ref, r_ref, h_ref):
    i = pl.program_id(0)
    j = pl.program_id(1)

    @pl.when(j == 0)
    def _():
        _norm_mod_to(h_ref, x_ref, mod_ref, g_ref, _mod_row(i), 0, D_MODEL)
        r_ref[...] = _dot(h_ref[...], wr_ref[...])

    o_ref[...] = _dot(h_ref[...], w_ref[...])


def _in_proj(x, mod_l, norm1_l, w_in, w_rank, l):
    return pl.pallas_call(
        _in_proj_kernel,
        grid=(N_TILES, PROJ_MAIN // TN_IN),
        in_specs=[
            pl.BlockSpec((TM, D_MODEL), lambda i, j: (i, 0)),
            pl.BlockSpec((8, N_MOD * D_MODEL), lambda i, j: (0, 0)),
            pl.BlockSpec((1, D_MODEL), lambda i, j: (0, 0)),
            pl.BlockSpec((None, D_MODEL, TN_IN), lambda i, j: (l, 0, j)),
            pl.BlockSpec((None, D_MODEL, 2 * GLA_RANK), lambda i, j: (l, 0, 0)),
        ],
        out_specs=[
            pl.BlockSpec((TM, TN_IN), lambda i, j: (i, j)),
            pl.BlockSpec((TM, 2 * GLA_RANK), lambda i, j: (i, 0)),
        ],
        out_shape=[
            jax.ShapeDtypeStruct((M_ALL, PROJ_MAIN), F32),
            jax.ShapeDtypeStruct((M_ALL, 2 * GLA_RANK), F32),
        ],
        scratch_shapes=[pltpu.VMEM((TM, D_MODEL), BF16)],
        compiler_params=pltpu.CompilerParams(
            dimension_semantics=("arbitrary", "arbitrary"), vmem_limit_bytes=VMEM_LIMIT),
        name="in_proj",
    )(x, mod_l, norm1_l, w_in, w_rank)


def _out_proj_kernel(a_ref, g_ref, x_ref, mod_ref, w_ref, o_ref):
    i = pl.program_id(0)
    j = pl.program_id(1)
    gate_row = mod_ref[pl.ds(_mod_row(i), 1), 2 * D_MODEL:3 * D_MODEL]
    gate = gate_row[:, :TN_OUT]
    for jj in range(1, D_MODEL // TN_OUT):
        gate = jnp.where(j == jj, gate_row[:, jj * TN_OUT:(jj + 1) * TN_OUT], gate)
    w = w_ref[...]
    mix = _dot(a_ref[...], w[:ATTN_WIDTH]) + _dot(g_ref[...], w[ATTN_WIDTH:])
    o_ref[...] = x_ref[...] + gate * mix


def _out_proj(attn, gla, x, mod_l, w_out, l):
    return pl.pallas_call(
        _out_proj_kernel,
        grid=(N_TILES, D_MODEL // TN_OUT),
        in_specs=[
            pl.BlockSpec((TM, ATTN_WIDTH), lambda i, j: (i, 0)),
            pl.BlockSpec((TM, GLA_VAL_WIDTH), lambda i, j: (i, 0)),
            pl.BlockSpec((TM, TN_OUT), lambda i, j: (i, j)),
            pl.BlockSpec((8, N_MOD * D_MODEL), lambda i, j: (0, 0)),
            pl.BlockSpec((None, D_MODEL, TN_OUT), lambda i, j: (l, 0, j)),
        ],
        out_specs=pl.BlockSpec((TM, TN_OUT), lambda i, j: (i, j)),
        out_shape=jax.ShapeDtypeStruct((M_ALL, D_MODEL), F32),
        compiler_params=pltpu.CompilerParams(
            dimension_semantics=("arbitrary", "arbitrary"), vmem_limit_bytes=VMEM_LIMIT),
        name="out_proj",
    )(attn, gla, x, mod_l, w_out)


def _mlp_kernel(x_ref, mod_ref, g_ref, wu_ref, wd_ref, o_ref, h_ref):
    i = pl.program_id(0)
    f = pl.program_id(1)
    row = _mod_row(i)

    @pl.when(f == 0)
    def _():
        _norm_mod_to(h_ref, x_ref, mod_ref, g_ref, row, 3 * D_MODEL, 4 * D_MODEL)
        o_ref[...] = jnp.zeros(o_ref.shape, F32)

    for rb in range(TM // MLP_ROWS):
        rows = slice(rb * MLP_ROWS, (rb + 1) * MLP_ROWS)
        u = _dot(h_ref[rows, :], wu_ref[...])
        u = jnp.square(jnp.maximum(u, 0.0)).astype(BF16)
        o_ref[rows, :] += _dot(u, wd_ref[...])

    @pl.when(f == D_FF // TF - 1)
    def _():
        gate = mod_ref[pl.ds(row, 1), 5 * D_MODEL:6 * D_MODEL]
        o_ref[...] = x_ref[...] + gate * o_ref[...]


def _mlp(x, mod_l, norm2_l, w_up, w_down, l):
    return pl.pallas_call(
        _mlp_kernel,
        grid=(N_TILES, D_FF // TF),
        in_specs=[
            pl.BlockSpec((TM, D_MODEL), lambda i, f: (i, 0), pipeline_mode=pl.Buffered(1)),
            pl.BlockSpec((8, N_MOD * D_MODEL), lambda i, f: (0, 0)),
            pl.BlockSpec((1, D_MODEL), lambda i, f: (0, 0)),
            pl.BlockSpec((None, D_MODEL, TF), lambda i, f: (l, 0, f)),
            pl.BlockSpec((None, TF, D_MODEL), lambda i, f: (l, f, 0)),
        ],
        out_specs=pl.BlockSpec((TM, D_MODEL), lambda i, f: (i, 0)),
        out_shape=jax.ShapeDtypeStruct((M_ALL, D_MODEL), F32),
        scratch_shapes=[pltpu.VMEM((TM, D_MODEL), BF16)],
        compiler_params=pltpu.CompilerParams(
            dimension_semantics=("arbitrary", "arbitrary"), vmem_limit_bytes=VMEM_LIMIT),
        name="mlp",
    )(x, mod_l, norm2_l, w_up, w_down)


def _rope(x, cos, sin_signed):
    lane = lax.broadcasted_iota(jnp.int32, x.shape, 1)
    first = (lane & 63) < ROPE_FREQS
    partner = jnp.where(first, pltpu.roll(x, HEAD_DIM - ROPE_FREQS, 1), pltpu.roll(x, ROPE_FREQS, 1))
    return x * cos + partner * sin_signed


def _softmax_pv(scores, values):
    m = functools.reduce(jnp.maximum, [jnp.max(s, axis=-1, keepdims=True) for s in scores])
    ps = [jnp.exp(s - m) for s in scores]
    denom = functools.reduce(jnp.add, [jnp.sum(p, axis=-1, keepdims=True) for p in ps])
    acc = functools.reduce(jnp.add, [_dot(p.astype(BF16), v) for p, v in zip(ps, values)])
    return acc / denom


def _attn_kernel(q_ref, kvc_ref, kvl_ref, ck_ref, cv_ref, qn_ref, kn_ref, cos_ref, sin_ref,
                 o_ref, knew_ref, klat_ref, vlat_ref):
    s = pl.program_id(0)
    qn = qn_ref[...]
    kn = kn_ref[...]

    def head(x, h):
        return x[:, h * HEAD_DIM:(h + 1) * HEAD_DIM]

    def store_heads(o, g2):
        for a in range(GQA_GROUP):
            h = g2 * GQA_GROUP + a
            o_ref[:, h * HEAD_DIM:(h + 1) * HEAD_DIM] = o[a * ATT_Q:(a + 1) * ATT_Q].astype(BF16)

    @pl.when(s < N_ATT_CTX)
    def _():
        q = q_ref[...]
        kv = kvc_ref[...]
        for g2 in range(KV_HEADS):
            kh = _rms(head(kv, g2), kn)
            knew_ref[:, g2 * HEAD_DIM:(g2 + 1) * HEAD_DIM] = kh
            vh = head(kv, KV_HEADS + g2).astype(BF16)
            qs = jnp.concatenate(
                [_rms(head(q, g2 * GQA_GROUP + a), qn) * ATT_SCALE for a in range(GQA_GROUP)],
                axis=0).astype(BF16)
            sc = _dot_nt(qs, kh.astype(BF16))
            store_heads(_softmax_pv([sc], [vh]), g2)

    @pl.when(s >= N_ATT_CTX)
    def _():
        qb = (s - N_ATT_CTX) % (T_LAT // ATT_Q)

        @pl.when(qb == 0)
        def _():
            kv = kvl_ref[...]
            cos = cos_ref[...]
            sin = sin_ref[...]
            for g2 in range(KV_HEADS):
                kh = _rope(_rms(head(kv, g2), kn), cos, sin)
                klat_ref[:, g2 * HEAD_DIM:(g2 + 1) * HEAD_DIM] = kh.astype(BF16)
            vlat_ref[...] = kv[:, KV_WIDTH:].astype(BF16)

        q0 = pl.multiple_of(qb * ATT_Q, ATT_Q)
        cos = cos_ref[pl.ds(q0, ATT_Q), :]
        sin = sin_ref[pl.ds(q0, ATT_Q), :]
        q = q_ref[...]
        for g2 in range(KV_HEADS):
            qs = jnp.concatenate(
                [_rope(_rms(head(q, g2 * GQA_GROUP + a), qn), cos, sin) * ATT_SCALE
                 for a in range(GQA_GROUP)], axis=0).astype(BF16)
            ck = head(ck_ref[...], g2).astype(BF16)
            cv = head(cv_ref[...], g2).astype(BF16)
            kl = klat_ref[:, g2 * HEAD_DIM:(g2 + 1) * HEAD_DIM]
            vl = vlat_ref[:, g2 * HEAD_DIM:(g2 + 1) * HEAD_DIM]
            store_heads(_softmax_pv([_dot_nt(qs, ck), _dot_nt(qs, kl)], [cv, vl]), g2)


def _lat_seq(s):
    return jnp.maximum(s - N_ATT_CTX, 0) // (T_LAT // ATT_Q)


def _attention(proj, cache_k, cache_v, q_norm_l, k_norm_l, cos, sin, l):
    kv_col = (ATTN_WIDTH) // (2 * KV_WIDTH)
    return pl.pallas_call(
        _attn_kernel,
        grid=(N_ATT_CTX + N_ATT_LAT,),
        in_specs=[
            pl.BlockSpec((ATT_Q, ATTN_WIDTH), lambda s: (s, 0)),
            pl.BlockSpec((T_CTX, 2 * KV_WIDTH), lambda s: (jnp.minimum(s, N_ATT_CTX - 1), kv_col)),
            pl.BlockSpec((T_LAT, 2 * KV_WIDTH), lambda s: (M_CTX // T_LAT + _lat_seq(s), kv_col)),
            pl.BlockSpec((None, None, PAST_LEN, KV_WIDTH), lambda s: (_lat_seq(s), l, 0, 0)),
            pl.BlockSpec((None, None, PAST_LEN, KV_WIDTH), lambda s: (_lat_seq(s), l, 0, 0)),
            pl.BlockSpec((1, HEAD_DIM), lambda s: (0, 0)),
            pl.BlockSpec((1, HEAD_DIM), lambda s: (0, 0)),
            pl.BlockSpec((T_LAT, HEAD_DIM), lambda s: (0, 0)),
            pl.BlockSpec((T_LAT, HEAD_DIM), lambda s: (0, 0)),
        ],
        out_specs=[
            pl.BlockSpec((ATT_Q, ATTN_WIDTH), lambda s: (s, 0)),
            pl.BlockSpec((T_CTX, KV_WIDTH), lambda s: (jnp.minimum(s, N_ATT_CTX - 1), 0)),
        ],
        out_shape=[
            jax.ShapeDtypeStruct((M_ALL, ATTN_WIDTH), BF16),
            jax.ShapeDtypeStruct((M_CTX, KV_WIDTH), F32),
        ],
        scratch_shapes=[pltpu.VMEM((T_LAT, KV_WIDTH), BF16), pltpu.VMEM((T_LAT, KV_WIDTH), BF16)],
        compiler_params=pltpu.CompilerParams(
            dimension_semantics=("arbitrary",), vmem_limit_bytes=VMEM_LIMIT),
        name="attention",
    )(proj, proj, proj, cache_k, cache_v, q_norm_l, k_norm_l, cos, sin)


def _slab_total(c, h, r8):
    z = jnp.where((r8 & (2 * h - 1)) == h - 1, c, 0.0)
    if h == 1:
        return z + pltpu.roll(z, 1, 0)
    if h == 2:
        z = z + pltpu.roll(z, SLAB - 1, 0)
        return z + pltpu.roll(z, 2, 0)
    z = z + pltpu.roll(z, SLAB - 1, 0)
    z = z + pltpu.roll(z, SLAB - 2, 0)
    return z + pltpu.roll(z, 4, 0)


def _decay_levels(g, backward):
    n = g.shape[1]
    n_slabs = CHUNK // SLAB
    gs = [g[SLAB * b:SLAB * (b + 1)] for b in range(n_slabs)]
    r8 = lax.broadcasted_iota(jnp.int32, (SLAB, n), 0)
    cs = list(gs)
    levels = []
    for h in (1, 2, 4):
        upper = (r8 & h) != 0
        e_h, nxt = [], []
        for b in range(n_slabs):
            tot_h = _slab_total(cs[b], h, r8)
            a = cs[b] - gs[b] if backward else cs[b]
            e_h.append(jnp.exp2(jnp.where(upper, a, tot_h - a)))
            nxt.append(cs[b] + jnp.where(upper, tot_h, 0.0))
        cs = nxt
        levels.append(jnp.concatenate(e_h, axis=0))
    for h in (8, 16, 32):
        half = h // SLAB
        e_h, nxt = [None] * n_slabs, [None] * n_slabs
        for blk in range(n_slabs // (2 * half)):
            first = blk * 2 * half
            tot_h = cs[first + half - 1][SLAB - 1:SLAB, :]
            for b in range(first, first + 2 * half):
                a = cs[b] - gs[b] if backward else cs[b]
                if b < first + half:
                    e_h[b] = jnp.exp2(tot_h - a)
                    nxt[b] = cs[b]
                else:
                    e_h[b] = jnp.exp2(a)
                    nxt[b] = cs[b] + tot_h
        cs = nxt
        levels.append(jnp.concatenate(e_h, axis=0))
    total = cs[n_slabs - 1][SLAB - 1:SLAB, :]
    prefix = jnp.concatenate([cs[b] - gs[b] if backward else cs[b] for b in range(n_slabs)], axis=0)
    return levels, prefix, total


def _store_block_diag(ref, lvl, kb):
    for grp in range(N_GROUPS):
        for hl in range(HEADS_PER_GROUP):
            lo = grp * GROUP_K + hl * GLA_DK
            ref[lvl, grp, hl * CHUNK:(hl + 1) * CHUNK, hl * GLA_DK:(hl + 1) * GLA_DK] = kb[:, lo:lo + GLA_DK]


def _gla_intra(c, refs, masks):
    (q_ref, k_ref, v_refs, r_ref, wpad_ref, bias, oacc_ref, kbd_ref, vbd_ref, qe_ref, kd_ref, vb_ref, dec_ref) = refs
    lvl, below, above = masks
    r0 = pl.multiple_of(c * CHUNK, CHUNK)
    x = _dot(r_ref[pl.ds(r0, CHUNK), :].astype(BF16), wpad_ref[...]) + bias
    g_all = (jnp.minimum(x, 0.0) - jnp.log(1.0 + jnp.exp(-jnp.abs(x)))) * (LOG2_E / GLA_TAU)

    q = q_ref[pl.ds(r0, CHUNK), :] * (GLA_DK ** -0.5)
    k = k_ref[pl.ds(r0, CHUNK), :]
    for grp in range(N_GROUPS):
        vb = v_refs[grp][pl.ds(r0, CHUNK), :].astype(BF16)
        vb_ref[pl.ds(r0, CHUNK), grp * GROUP_V:(grp + 1) * GROUP_V] = vb
        for hl in range(HEADS_PER_GROUP):
            vbd_ref[grp, hl * CHUNK:(hl + 1) * CHUNK, hl * GLA_DV:(hl + 1) * GLA_DV] = (
                vb[:, hl * GLA_DV:(hl + 1) * GLA_DV])

    def group(x, grp):
        return x[:, grp * GROUP_K:(grp + 1) * GROUP_K]

    _store_block_diag(kbd_ref, 0, k.astype(BF16))
    qb = q.astype(BF16)
    diag = [_dot_nt(group(qb, grp), kbd_ref[0, grp]) for grp in range(N_GROUPS)]

    att = []
    for d, backward in enumerate((False, True)):
        g = g_all[:, d * GLA_KEY_WIDTH:(d + 1) * GLA_KEY_WIDTH]
        levels, prefix, total = _decay_levels(g, backward)
        e_query = jnp.exp2(total - prefix) if backward else jnp.exp2(prefix)
        e_key = jnp.exp2(prefix) if backward else jnp.exp2(total - prefix)
        qe_ref[d, pl.ds(r0, CHUNK), :] = (q * e_query).astype(BF16)
        kd_ref[d, pl.ds(r0, CHUNK), :] = (k * e_key).astype(BF16)
        dec_ref[d, c] = jnp.broadcast_to(jnp.exp2(total), (SLAB, GLA_KEY_WIDTH))
        sc = list(diag)
        for idx, h in enumerate(LEVELS):
            slot = 1 + d * len(LEVELS) + idx
            e = levels[idx]
            _store_block_diag(kbd_ref, slot, (k * e).astype(BF16))
            qh = (q * e).astype(BF16)
            for grp in range(N_GROUPS):
                sc[grp] = jnp.where(lvl >= h, _dot_nt(group(qh, grp), kbd_ref[slot, grp]), sc[grp])
        att.append(sc)

    for grp in range(N_GROUPS):
        both = jnp.where(below, att[0][grp], jnp.where(above, att[1][grp], att[0][grp] + att[1][grp]))
        oacc_ref[pl.ds(r0, CHUNK), grp * GROUP_V:(grp + 1) * GROUP_V] = _dot(both.astype(BF16), vbd_ref[grp])


def _gla_inter(c, d, refs, pair_mask):
    (oacc_ref, st_ref, qe_ref, kd_ref, vb_ref, dec_ref) = refs
    r0 = pl.multiple_of(c * CHUNK, CHUNK)
    for p in range(N_PAIRS):
        st = st_ref[d, p]
        qe = qe_ref[d, pl.ds(r0, CHUNK), p * PAIR_K:(p + 1) * PAIR_K]
        oacc_ref[pl.ds(r0, CHUNK), p * PAIR_V:(p + 1) * PAIR_V] += _dot_nt(qe, st.astype(BF16))
        kd = kd_ref[d, pl.ds(r0, CHUNK), p * PAIR_K:(p + 1) * PAIR_K]
        vb = vb_ref[pl.ds(r0, CHUNK), p * PAIR_V:(p + 1) * PAIR_V]
        decay = dec_ref[d, c, 0:1, p * PAIR_K:(p + 1) * PAIR_K]
        st_ref[d, p] = st * decay + jnp.where(pair_mask, _dot_tn(vb, kd), 0.0)


def _gla_kernel(q_ref, k_ref, v0_ref, v1_ref, og0_ref, og1_ref, r_ref, wgf_ref, wgb_ref,
                bgf_ref, bgb_ref, gn_ref, s0f_ref, s0b_ref,
                o_ref, sf_ref, sb_ref,
                oacc_ref, st_ref, wpad_ref, kbd_ref, vbd_ref, qe_ref, kd_ref, vb_ref, dec_ref):
    s = pl.program_id(0)
    is_ctx = s < N_CTX_TILES

    def iota2(shape, dim):
        return lax.broadcasted_iota(jnp.int32, shape, dim)

    ii = iota2((CHUNK, GROUP_K), 0)
    jj = iota2((CHUNK, GROUP_K), 1) & (CHUNK - 1)
    masks = (ii ^ jj, jj < ii, jj > ii)
    pair_mask = iota2((PAIR_V, PAIR_K), 0) // GLA_DV == iota2((PAIR_V, PAIR_K), 1) // GLA_DK
    first_head = iota2((GLA_DV, PAIR_K), 1) < GLA_DK

    @pl.when(s == 0)
    def _():
        kbd_ref[...] = jnp.zeros(kbd_ref.shape, BF16)
        vbd_ref[...] = jnp.zeros(vbd_ref.shape, BF16)

    zeros = jnp.zeros((GLA_RANK, GLA_KEY_WIDTH), F32)
    wpad_ref[...] = jnp.concatenate(
        [jnp.concatenate([wgf_ref[...], zeros], axis=0), jnp.concatenate([zeros, wgb_ref[...]], axis=0)],
        axis=1).astype(BF16)
    bias = jnp.concatenate([bgf_ref[...], bgb_ref[...]], axis=1)

    intra_refs = (q_ref, k_ref, (v0_ref, v1_ref), r_ref, wpad_ref, bias, oacc_ref, kbd_ref, vbd_ref,
                  qe_ref, kd_ref, vb_ref, dec_ref)

    def intra_body(c, carry):
        _gla_intra(c, intra_refs, masks)
        return carry

    lax.fori_loop(0, N_CHUNKS, intra_body, 0)

    @pl.when(jnp.logical_not(is_ctx))
    def _():
        for d, s0_ref in enumerate((s0f_ref, s0b_ref)):
            for p in range(N_PAIRS):
                both = s0_ref[p]
                st_ref[d, p, 0:GLA_DV, :] = jnp.where(first_head, both, 0.0)
                st_ref[d, p, GLA_DV:PAIR_V, :] = jnp.where(first_head, 0.0, both)

    inter_refs = (oacc_ref, st_ref, qe_ref, kd_ref, vb_ref, dec_ref)

    def inter_body(t, carry):
        for d, out_ref in enumerate((sf_ref, sb_ref)):
            backward = d == 1
            c = (N_CHUNKS - 1 - t) if backward else t
            pos = c % CHUNKS_PER_CTX_SEQ
            first_of_seq = pos == ((CHUNKS_PER_CTX_SEQ - 1) if backward else 0)
            last_of_seq = pos == (0 if backward else (CHUNKS_PER_CTX_SEQ - 1))

            @pl.when(jnp.logical_and(is_ctx, first_of_seq))
            def _():
                st_ref[d] = jnp.zeros(st_ref.shape[1:], F32)

            _gla_inter(c, d, inter_refs, pair_mask)

            @pl.when(jnp.logical_and(is_ctx, last_of_seq))
            def _():
                seq = c // CHUNKS_PER_CTX_SEQ
                for p in range(N_PAIRS):
                    out_ref[seq, p] = jnp.where(first_head, st_ref[d, p, 0:GLA_DV, :], st_ref[d, p, GLA_DV:PAIR_V, :])

        return carry

    lax.fori_loop(0, N_CHUNKS, inter_body, 0)

    gn = gn_ref[...]

    def epilogue(t, carry):
        r0 = pl.multiple_of(t * GATE_ROWS, GATE_ROWS)
        for h in range(GLA_HEADS):
            og_ref = og0_ref if h < HEADS_PER_GROUP else og1_ref
            c0 = (h % HEADS_PER_GROUP) * GLA_DV
            gate = og_ref[pl.ds(r0, GATE_ROWS), c0:c0 + GLA_DV]
            o = oacc_ref[pl.ds(r0, GATE_ROWS), h * GLA_DV:(h + 1) * GLA_DV]
            y = _rms(o, gn) * (gate * jax.nn.sigmoid(gate))
            o_ref[pl.ds(r0, GATE_ROWS), h * GLA_DV:(h + 1) * GLA_DV] = y.astype(BF16)
        return carry

    lax.fori_loop(0, TM // GATE_ROWS, epilogue, 0, unroll=GATE_UNROLL)


def _gla(proj, r, wgf, wgb, bgf, bgb, gn, s0f, s0b, l):
    def col(cb):
        return pl.BlockSpec((TM, PROJ_COL), lambda s: (s, cb))

    base = (ATTN_WIDTH + 2 * KV_WIDTH) // PROJ_COL
    seqs_per_tile = TM // T_CTX

    def lat(s):
        return jnp.maximum(s - N_CTX_TILES, 0)

    def ctx(s):
        return jnp.minimum(s, N_CTX_TILES - 1)

    state_in = pl.BlockSpec((None, N_PAIRS, GLA_DV, PAIR_K), lambda s: (lat(s), 0, 0, 0))
    state_out = pl.BlockSpec((seqs_per_tile, N_PAIRS, GLA_DV, PAIR_K), lambda s: (ctx(s), 0, 0, 0))
    state_shape = jax.ShapeDtypeStruct((N_CTX_SEQ, N_PAIRS, GLA_DV, PAIR_K), F32)
    n_slots = 1 + 2 * len(LEVELS)
    return pl.pallas_call(
        _gla_kernel,
        grid=(N_TILES,),
        in_specs=[
            col(base), col(base + 1), col(base + 2), col(base + 3), col(base + 4), col(base + 5),
            pl.BlockSpec((TM, 2 * GLA_RANK), lambda s: (s, 0)),
            pl.BlockSpec((None, GLA_RANK, GLA_KEY_WIDTH), lambda s: (l, 0, 0)),
            pl.BlockSpec((None, GLA_RANK, GLA_KEY_WIDTH), lambda s: (l, 0, 0)),
            pl.BlockSpec((None, 1, GLA_KEY_WIDTH), lambda s: (l, 0, 0)),
            pl.BlockSpec((None, 1, GLA_KEY_WIDTH), lambda s: (l, 0, 0)),
            pl.BlockSpec((1, GLA_DV), lambda s: (0, 0)),
            state_in, state_in,
        ],
        out_specs=[pl.BlockSpec((TM, GLA_VAL_WIDTH), lambda s: (s, 0)), state_out, state_out],
        out_shape=[jax.ShapeDtypeStruct((M_ALL, GLA_VAL_WIDTH), BF16), state_shape, state_shape],
        scratch_shapes=[
            pltpu.VMEM((TM, GLA_VAL_WIDTH), F32),
            pltpu.VMEM((2, N_PAIRS, PAIR_V, PAIR_K), F32),
            pltpu.VMEM((2 * GLA_RANK, 2 * GLA_KEY_WIDTH), BF16),
            pltpu.VMEM((n_slots, N_GROUPS, GROUP_K, GROUP_K), BF16),
            pltpu.VMEM((N_GROUPS, GROUP_K, GROUP_V), BF16),
            pltpu.VMEM((2, TM, GLA_KEY_WIDTH), BF16),
            pltpu.VMEM((2, TM, GLA_KEY_WIDTH), BF16),
            pltpu.VMEM((TM, GLA_VAL_WIDTH), BF16),
            pltpu.VMEM((2, N_CHUNKS, SLAB, GLA_KEY_WIDTH), F32),
        ],
        compiler_params=pltpu.CompilerParams(
            dimension_semantics=("arbitrary",), vmem_limit_bytes=VMEM_LIMIT),
        name="gla",
    )(proj, proj, proj, proj, proj, proj, r, wgf, wgb, bgf, bgb, gn, s0f, s0b)


def _rope_tables():
    t = jnp.arange(T_LAT)
    inv = ROPE_THETA ** (-jnp.arange(ROPE_FREQS, dtype=F32) / ROPE_FREQS)
    ang_r = (t // GRID_W).astype(F32)[:, None] * inv[None, :]
    ang_c = (t % GRID_W).astype(F32)[:, None] * inv[None, :]
    cos = jnp.concatenate([jnp.cos(ang_r), jnp.cos(ang_r), jnp.cos(ang_c), jnp.cos(ang_c)], axis=-1)
    sin = jnp.concatenate([-jnp.sin(ang_r), jnp.sin(ang_r), -jnp.sin(ang_c), jnp.sin(ang_c)], axis=-1)
    return cos, sin


def _state_to_pairs(state_l):
    st = state_l.reshape(-1, N_PAIRS, 2, GLA_DK, GLA_DV)
    return st.transpose(0, 1, 4, 2, 3).reshape(-1, N_PAIRS, GLA_DV, PAIR_K)


def _pairs_to_state(st):
    st = st.reshape(-1, N_PAIRS, GLA_DV, 2, GLA_DK)
    return st.transpose(0, 1, 3, 4, 2).reshape(-1, GLA_HEADS, GLA_DK, GLA_DV)


def kernel(x_prompt, x_sample, cache_k, cache_v, state_gla_fwd, state_gla_bwd, c, c_ctx, w_mod, b_mod,
           norm1, w_in, q_norm, k_norm, w_gate_fwd, b_gate_fwd, w_gate_bwd, b_gate_bwd, gla_norm,
           w_out, norm2, w_up, w_down):
    x = jnp.concatenate([x_prompt.reshape(M_CTX, D_MODEL), x_sample.reshape(M_LAT, D_MODEL)], axis=0)
    cvec = jnp.concatenate([c_ctx[None, :], c, jnp.zeros((8 - 1 - N_LAT_SEQ, D_MODEL), F32)], axis=0)
    mod = _modulation(cvec, w_mod, b_mod)
    cos, sin = _rope_tables()
    w_main = w_in[:, :, :PROJ_MAIN].astype(BF16)
    w_rank = w_in[:, :, PROJ_MAIN:].astype(BF16)
    w_out_b, w_up_b, w_down_b = w_out.astype(BF16), w_up.astype(BF16), w_down.astype(BF16)
    ck = cache_k.reshape(N_LAT_SEQ, DEPTH, PAST_LEN, KV_WIDTH)
    cv = cache_v.reshape(N_LAT_SEQ, DEPTH, PAST_LEN, KV_WIDTH)

    ks, vs, sfs, sbs = [], [], [], []
    for l in range(DEPTH):
        proj, r = _in_proj(x, mod[l], norm1[l][None, :], w_main, w_rank, l)
        attn, k_new = _attention(proj, ck, cv, q_norm[l][None, :], k_norm[l][None, :], cos, sin, l)
        gla, sf, sb = _gla(proj, r, w_gate_fwd, w_gate_bwd,
                           b_gate_fwd.reshape(DEPTH, 1, GLA_KEY_WIDTH), b_gate_bwd.reshape(DEPTH, 1, GLA_KEY_WIDTH),
                           gla_norm[l][None, :], _state_to_pairs(state_gla_fwd[:, l]),
                           _state_to_pairs(state_gla_bwd[:, l]), l)
        x = _out_proj(attn, gla, x, mod[l], w_out_b, l)
        x = _mlp(x, mod[l], norm2[l][None, :], w_up_b, w_down_b, l)
        ks.append(k_new.reshape(N_CTX_SEQ, T_CTX, KV_HEADS, HEAD_DIM))
        vs.append(proj[:M_CTX, ATTN_WIDTH + KV_WIDTH:ATTN_WIDTH + 2 * KV_WIDTH]
                  .reshape(N_CTX_SEQ, T_CTX, KV_HEADS, HEAD_DIM))
        sfs.append(_pairs_to_state(sf))
        sbs.append(_pairs_to_state(sb))

    y_p = x[:M_CTX].reshape(N_CTX_SEQ, T_CTX, D_MODEL)
    y_s = x[M_CTX:].reshape(N_LAT_SEQ, T_LAT, D_MODEL)
    return (y_p, y_s, jnp.stack(ks, axis=1), jnp.stack(vs, axis=1),
            jnp.stack(sfs, axis=1), jnp.stack(sbs, axis=1))
```
